```python
import math
import jax, jax.numpy as jnp
from jax import lax
import numpy as np

D_MODEL = 1024
BATCH = 8
SEQ = 4096
DEPTH = 1

CHUNK = 64
D_MIX = D_MODEL
D_A = D_MIX // 2
D_B = D_MIX - D_A
H_A = 8
DH_A = D_A // H_A
H_B = 8
DH_B = D_B // H_B
SGU_BLOCK = 128
Q_BLOCK = 128
D_FF = 2816
N_SUB = 3
OFF_Z_A = 0
OFF_Q = 2 * D_A
OFF_K = OFF_Q + D_B
OFF_V = OFF_K + D_B
OFF_F = OFF_V + D_B
N_IN = OFF_F + H_B
EPS = 1e-6
NEG_INF = -1e30

kernel_name = 'hybrid_sgu_fox_macaron_block'


def rms_norm(x, g):
    xf = x.astype(jnp.float32)
    y = xf * lax.rsqrt(jnp.mean(xf * xf, axis=-1, keepdims=True) + EPS)
    return (y * g.astype(jnp.float32)).astype(x.dtype)


def layer_norm(x, g, b):
    xf = x.astype(jnp.float32)
    mu = jnp.mean(xf, axis=-1, keepdims=True)
    xc = xf - mu
    y = xc * lax.rsqrt(jnp.mean(xc * xc, axis=-1, keepdims=True) + EPS)
    return (y * g.astype(jnp.float32) + b.astype(jnp.float32)).astype(x.dtype)


def modulate(x, g_pre, shift, scale):
    return rms_norm(x, g_pre) * (1 + scale[:, None, :]) + shift[:, None, :]


def swiglu(h, w_gate, w_up, w_down):
    return (jax.nn.silu(h @ w_gate) * (h @ w_up)) @ w_down


def spatial_gating(z, ln_g, ln_b, w_s, b_s):
    b_, s_, _ = z.shape
    z = jax.nn.gelu(z)
    u, v = z[..., :D_A], z[..., D_A:]
    v = layer_norm(v, ln_g, ln_b)
    v = v.reshape(b_, s_ // SGU_BLOCK, SGU_BLOCK, H_A, DH_A)
    pos = jnp.arange(SGU_BLOCK)
    mask = (pos[None, :] // CHUNK) <= (pos[:, None] // CHUNK)
    w = jnp.where(mask[None], w_s, jnp.zeros((), w_s.dtype))
    gate = jnp.einsum('hij,bnjhc->bnihc', w, v) + b_s.T[None, None, :, :, None]
    return u * gate.reshape(b_, s_, D_A)


def forgetting_attention(q, k, v, log_f):
    s_ = q.shape[2]
    cum = jnp.cumsum(log_f, axis=-1)
    scale = DH_B ** -0.5
    outs = []
    for blk in range(s_ // Q_BLOCK):
        q0 = blk * Q_BLOCK
        q1 = q0 + Q_BLOCK
        logits = jnp.einsum('bhqd,bhkd->bhqk', q[:, :, q0:q1], k[:, :, :q1]).astype(jnp.float32) * scale
        logits = logits + cum[:, :, q0:q1, None] - cum[:, :, None, :q1]
        qpos = q0 + jnp.arange(Q_BLOCK)
        kpos = jnp.arange(q1)
        logits = jnp.where(kpos[None, :] <= qpos[:, None], logits, NEG_INF)
        p = jax.nn.softmax(logits, axis=-1).astype(v.dtype)
        outs.append(jnp.einsum('bhqk,bhkd->bhqd', p, v[:, :, :q1]))
    return jnp.concatenate(outs, axis=2)


def hybrid_mixer(h, w_in, sgu_ln_g, sgu_ln_b, sgu_w, sgu_b, fox_b_f, gnorm_a_g, gnorm_b_g, w_out):
    b_, s_, _ = h.shape
    proj = h @ w_in
    y_a = spatial_gating(proj[..., OFF_Z_A:OFF_Q], sgu_ln_g, sgu_ln_b, sgu_w, sgu_b)

    def heads(t):
        return t.reshape(b_, s_, H_B, DH_B).transpose(0, 2, 1, 3)

    q = heads(proj[..., OFF_Q:OFF_K])
    k = heads(proj[..., OFF_K:OFF_V])
    v = heads(proj[..., OFF_V:OFF_F])
    log_f = jax.nn.log_sigmoid(proj[..., OFF_F:N_IN].astype(jnp.float32)
                               + fox_b_f.astype(jnp.float32)).transpose(0, 2, 1)
    y_b = forgetting_attention(q, k, v, log_f).transpose(0, 2, 1, 3).reshape(b_, s_, D_B)
    y = jnp.concatenate([rms_norm(y_a, gnorm_a_g), rms_norm(y_b, gnorm_b_g)], axis=-1)
    return y @ w_out


def setup_inputs(seed: int = 0) -> dict:
    key = jax.random.key(seed)
    ks = jax.random.split(key, 20)
    f32 = jnp.float32
    n = lambda k, shape, s: jax.random.normal(k, shape, f32) * s
    return {
        'x': n(ks[0], (BATCH, SEQ, D_MODEL), 1.0),
        'c': n(ks[1], (BATCH, D_MODEL), 1.0),
        'w_ada': n(ks[2], (DEPTH, D_MODEL, N_SUB * 3 * D_MODEL), 0.1 * D_MODEL ** -0.5),
        'b_ada': n(ks[3], (DEPTH, N_SUB * 3 * D_MODEL), 0.01),
        'norm_pre_g': 1.0 + n(ks[4], (DEPTH, N_SUB, D_MODEL), 0.02),
        'norm_post_g': 1.0 + n(ks[5], (DEPTH, N_SUB, D_MODEL), 0.02),
        'ffn_w_gate': n(ks[6], (DEPTH, 2, D_MODEL, D_FF), D_MODEL ** -0.5),
        'ffn_w_up': n(ks[7], (DEPTH, 2, D_MODEL, D_FF), D_MODEL ** -0.5),
        'ffn_w_down': n(ks[8], (DEPTH, 2, D_FF, D_MODEL), D_FF ** -0.5),
        'w_in': n(ks[9], (DEPTH, D_MODEL, N_IN), D_MODEL ** -0.5),
        'sgu_ln_g': 1.0 + n(ks[10], (DEPTH, D_A), 0.02),
        'sgu_ln_b': n(ks[11], (DEPTH, D_A), 0.02),
        'sgu_w': n(ks[12], (DEPTH, H_A, SGU_BLOCK, SGU_BLOCK), SGU_BLOCK ** -0.5),
        'sgu_b': 1.0 + n(ks[13], (DEPTH, H_A, SGU_BLOCK), 0.02),
        'fox_b_f': 2.0 + n(ks[14], (DEPTH, H_B), 0.5),
        'gnorm_a_g': 1.0 + n(ks[15], (DEPTH, D_A), 0.02),
        'gnorm_b_g': 1.0 + n(ks[16], (DEPTH, D_B), 0.02),
        'w_out': n(ks[17], (DEPTH, D_MIX, D_MODEL), D_MIX ** -0.5),
    }


def reference(x, c, w_ada, b_ada, norm_pre_g, norm_post_g, ffn_w_gate, ffn_w_up, ffn_w_down,
              w_in, sgu_ln_g, sgu_ln_b, sgu_w, sgu_b, fox_b_f, gnorm_a_g, gnorm_b_g, w_out):
    b_ = x.shape[0]
    for l in range(DEPTH):
        mod = (jax.nn.silu(c) @ w_ada[l] + b_ada[l]).reshape(b_, N_SUB, 3, D_MODEL)
        shift, scale, gate = mod[:, :, 0], mod[:, :, 1], mod[:, :, 2]

        h = modulate(x, norm_pre_g[l, 0], shift[:, 0], scale[:, 0])
        h = swiglu(h, ffn_w_gate[l, 0], ffn_w_up[l, 0], ffn_w_down[l, 0])
        x = x + 0.5 * (1 + gate[:, 0, None, :]) * rms_norm(h, norm_post_g[l, 0])

        h = modulate(x, norm_pre_g[l, 1], shift[:, 1], scale[:, 1])
        h = hybrid_mixer(h, w_in[l], sgu_ln_g[l], sgu_ln_b[l], sgu_w[l], sgu_b[l], fox_b_f[l],
                         gnorm_a_g[l], gnorm_b_g[l], w_out[l])
        x = x + (1 + gate[:, 1, None, :]) * rms_norm(h, norm_post_g[l, 1])

        h = modulate(x, norm_pre_g[l, 2], shift[:, 2], scale[:, 2])
        h = swiglu(h, ffn_w_gate[l, 1], ffn_w_up[l, 1], ffn_w_down[l, 1])
        x = x + 0.5 * (1 + gate[:, 2, None, :]) * rms_norm(h, norm_post_g[l, 2])
    return x
```

```python
import functools
import math

import jax
import jax.numpy as jnp
from jax import lax
from jax.experimental import pallas as pl
from jax.experimental.pallas import tpu as pltpu

F32 = jnp.float32
BF16 = jnp.bfloat16

CHUNK = 64
H_A = 8
H_B = 8
SGU_BLOCK = 128
N_SUB = 3
EPS = 1e-6
NEG_INF = -1e30

LANES = 128
VMEM_LIMIT_BYTES = 56 * 1024 * 1024

TM_FFN = 512
FF_CHUNK = 256
TM_MIX = 512
TQ = 256


def _const_spec(shape):
    nd = len(shape)
    return pl.BlockSpec(shape, lambda *_: (0,) * nd, pipeline_mode=pl.Buffered(1))


def _rms_scale(v):
    return lax.rsqrt(jnp.mean(v * v, axis=-1, keepdims=True) + EPS)


def _modulated(x, mod_ref, gpre_ref):
    shift = mod_ref[0:1, :]
    scale = mod_ref[1:2, :]
    pre = gpre_ref[...] * (1.0 + scale)
    return (x * _rms_scale(x) * pre + shift).astype(BF16)


def _silu(v):
    return v * jax.nn.sigmoid(v)


def _gelu_tanh(v):
    c = math.sqrt(2.0 / math.pi)
    return 0.5 * v * (1.0 + jnp.tanh(c * (v + 0.044715 * (v * v * v))))


def _log_sigmoid(v):
    return jnp.minimum(v, 0.0) - jnp.log(1.0 + jnp.exp(-jnp.abs(v)))


def _adaln_kernel(c_ref, w_ref, b_ref, o_ref):
    sc = _silu(c_ref[...])
    o_ref[...] = jnp.dot(sc, w_ref[...], preferred_element_type=F32,
                         precision=lax.Precision.HIGHEST) + b_ref[...]


def _adaln(c, w, b):
    bsz, d = c.shape
    n = w.shape[1]
    tn = 1024
    return pl.pallas_call(
        _adaln_kernel,
        grid=(n // tn,),
        in_specs=[pl.BlockSpec((bsz, d), lambda j: (0, 0)),
                  pl.BlockSpec((d, tn), lambda j: (0, j)),
                  pl.BlockSpec((1, tn), lambda j: (0, j))],
        out_specs=pl.BlockSpec((bsz, tn), lambda j: (0, j)),
        out_shape=jax.ShapeDtypeStruct((bsz, n), F32),
        compiler_params=pltpu.CompilerParams(dimension_semantics=("arbitrary",)),
        name="adaln",
    )(c, w, b.reshape(1, n))


def _ffn_kernel(x_ref, mod_ref, gpre_ref, gpost_ref, wg_ref, wu_ref, wd_ref, o_ref, acc_ref,
                *, res_scale):
    x = x_ref[...]
    hb = _modulated(x, mod_ref, gpre_ref)
    d_ff = wg_ref.shape[1]
    for ci in range(d_ff // FF_CHUNK):
        cs = slice(ci * FF_CHUNK, (ci + 1) * FF_CHUNK)
        g = jnp.dot(hb, wg_ref[:, cs], preferred_element_type=F32)
        u = jnp.dot(hb, wu_ref[:, cs], preferred_element_type=F32)
        a = (_silu(g) * u).astype(BF16)
        part = jnp.dot(a, wd_ref[cs, :], preferred_element_type=F32)
        if ci == 0:
            acc_ref[...] = part
        else:
            acc_ref[...] += part
    y = acc_ref[...]
    yn = y * _rms_scale(y) * gpost_ref[...]
    gate = mod_ref[2:3, :]
    o_ref[...] = x + (res_scale * (1.0 + gate)) * yn


def _ffn(x, mod, sub, g_pre, g_post, wg, wu, wd, res_scale):
    bsz, seq, d = x.shape
    d_ff = wg.shape[1]
    tm = TM_FFN
    tok = pl.BlockSpec((None, tm, d), lambda b, s: (b, s, 0))
    return pl.pallas_call(
        functools.partial(_ffn_kernel, res_scale=res_scale),
        grid=(bsz, seq // tm),
        in_specs=[tok,
                  pl.BlockSpec((None, None, 3, d), lambda b, s: (b, sub, 0, 0)),
                  _const_spec((1, d)), _const_spec((1, d)),
                  _const_spec((d, d_ff)), _const_spec((d, d_ff)), _const_spec((d_ff, d))],
        out_specs=tok,
        out_shape=jax.ShapeDtypeStruct(x.shape, F32),
        scratch_shapes=[pltpu.VMEM((tm, d), F32)],
        compiler_params=pltpu.CompilerParams(
            dimension_semantics=("arbitrary", "arbitrary"),
            vmem_limit_bytes=VMEM_LIMIT_BYTES),
        name="ffn",
    )(x, mod, g_pre.reshape(1, d), g_post.reshape(1, d), wg, wu, wd)


def _mixin_kernel(x_ref, mod_ref, gpre_ref, wz_ref, wqkv_ref, wf_ref, bf_ref,
                  lng_ref, lnb_ref, sguw_ref, sgub_ref, gna_ref,
                  ya_ref, q_ref, k_ref, v_ref, cum_ref,
                  gate_ref, carry_ref, *, q_scale):
    tm = x_ref.shape[0]
    d_a = ya_ref.shape[1]
    d_b = q_ref.shape[1]
    hb = _modulated(x_ref[...], mod_ref, gpre_ref)

    z = _gelu_tanh(jnp.dot(hb, wz_ref[...], preferred_element_type=F32))
    u = z[:, :d_a]
    v = z[:, d_a:]
    vc = v - jnp.mean(v, axis=-1, keepdims=True)
    vn = (vc * lax.rsqrt(jnp.mean(vc * vc, axis=-1, keepdims=True) + EPS) * lng_ref[...]
          + lnb_ref[...]).astype(BF16)

    row = lax.broadcasted_iota(jnp.int32, (SGU_BLOCK, SGU_BLOCK), 0)
    col = lax.broadcasted_iota(jnp.int32, (SGU_BLOCK, SGU_BLOCK), 1)
    chunk_causal = (col // CHUNK) <= (row // CHUNK)
    low_half = col < (LANES // 2)
    zero = jnp.zeros((), BF16)
    for p in range(d_a // LANES):
        w_even = jnp.where(chunk_causal, sguw_ref[2 * p], 0.0).astype(BF16)
        w_odd = jnp.where(chunk_causal, sguw_ref[2 * p + 1], 0.0).astype(BF16)
        w_cat = jnp.concatenate([w_even, w_odd], axis=1)
        for blk in range(tm // SGU_BLOCK):
            rows = slice(blk * SGU_BLOCK, (blk + 1) * SGU_BLOCK)
            lanes = slice(p * LANES, (p + 1) * LANES)
            vt = vn[rows, lanes]
            rhs = jnp.concatenate([jnp.where(low_half, vt, zero),
                                   jnp.where(low_half, zero, vt)], axis=0)
            gate_ref[rows, lanes] = (jnp.dot(w_cat, rhs, preferred_element_type=F32)
                                     + sgub_ref[:, lanes])
    ya = u * gate_ref[...]
    ya_ref[...] = (ya * _rms_scale(ya) * gna_ref[...]).astype(BF16)

    qkv = jnp.dot(hb, wqkv_ref[...], preferred_element_type=F32)
    q_ref[...] = (qkv[:, :d_b] * q_scale).astype(BF16)
    k_ref[...] = qkv[:, d_b:2 * d_b].astype(BF16)
    v_ref[...] = qkv[:, 2 * d_b:].astype(BF16)

    @pl.when(pl.program_id(1) == 0)
    def _():
        carry_ref[...] = jnp.zeros_like(carry_ref)

    log_f = _log_sigmoid(jnp.dot(hb, wf_ref[...], preferred_element_type=F32) + bf_ref[...])
    r = lax.broadcasted_iota(jnp.int32, (tm, tm), 0)
    cidx = lax.broadcasted_iota(jnp.int32, (tm, tm), 1)
    tri = jnp.where(cidx <= r, 1.0, 0.0).astype(F32)
    cum = jnp.dot(tri, log_f, preferred_element_type=F32,
                  precision=lax.Precision.HIGHEST) + carry_ref[0:1, :]
    cum_ref[...] = cum
    carry_ref[0:1, :] = cum[tm - 1:tm, :]


def _mix_in(x, mod, g_pre, wz, wqkv, wf, bf, ln_g, ln_b, sgu_w, sgu_bias, gna):
    bsz, seq, d = x.shape
    d_a = wz.shape[1] // 2
    d_b = wqkv.shape[1] // 3
    tm = TM_MIX
    tok = lambda width: pl.BlockSpec((None, tm, width), lambda b, s: (b, s, 0))
    dh = d_b // H_B
    return pl.pallas_call(
        functools.partial(_mixin_kernel, q_scale=dh ** -0.5),
        grid=(bsz, seq // tm),
        in_specs=[tok(d),
                  pl.BlockSpec((None, None, 3, d), lambda b, s: (b, 1, 0, 0)),
                  _const_spec((1, d)),
                  _const_spec(wz.shape), _const_spec(wqkv.shape), _const_spec(wf.shape),
                  _const_spec((1, LANES)),
                  _const_spec((1, d_a)), _const_spec((1, d_a)),
                  _const_spec(sgu_w.shape), _const_spec(sgu_bias.shape),
                  _const_spec((1, d_a))],
        out_specs=[tok(d_a), tok(d_b), tok(d_b), tok(d_b), tok(LANES)],
        out_shape=[jax.ShapeDtypeStruct((bsz, seq, d_a), BF16),
                   jax.ShapeDtypeStruct((bsz, seq, d_b), BF16),
                   jax.ShapeDtypeStruct((bsz, seq, d_b), BF16),
                   jax.ShapeDtypeStruct((bsz, seq, d_b), BF16),
                   jax.ShapeDtypeStruct((bsz, seq, LANES), F32)],
        scratch_shapes=[pltpu.VMEM((tm, d_a), F32), pltpu.VMEM((8, LANES), F32)],
        compiler_params=pltpu.CompilerParams(
            dimension_semantics=("arbitrary", "arbitrary"),
            vmem_limit_bytes=VMEM_LIMIT_BYTES),
        name="mix_in",
    )(x, mod, g_pre.reshape(1, d), wz, wqkv, wf, bf, ln_g.reshape(1, d_a), ln_b.reshape(1, d_a),
      sgu_w, sgu_bias, gna.reshape(1, d_a))


def _attn_kernel(q_ref, k_ref, v_ref, cq_ref, ck_ref, o_ref):
    tq = q_ref.shape[0]
    hp = pl.program_id(1)
    qi = pl.program_id(2)
    lane = lax.broadcasted_iota(jnp.int32, (1, LANES), 1)
    q_pair = q_ref[...]
    cq_all = cq_ref[...]
    rpos = lax.broadcasted_iota(jnp.int32, (tq, tq), 0)
    cpos = lax.broadcasted_iota(jnp.int32, (tq, tq), 1)
    zero = jnp.zeros((), BF16)
    outs = []
    for i in range(2):
        head = 2 * hp + i
        in_head = (lane < LANES // 2) if i == 0 else (lane >= LANES // 2)
        qh = jnp.where(in_head, q_pair, zero)
        cq = jnp.sum(jnp.where(lane == head, cq_all, 0.0), axis=-1, keepdims=True)

        def step(j, carry, masked):
            m, l, acc = carry
            kb = k_ref[pl.ds(pl.multiple_of(j * tq, tq), tq), :]
            vb = jnp.where(in_head, v_ref[pl.ds(pl.multiple_of(j * tq, tq), tq), :], zero)
            ck = ck_ref[head, pl.ds(j, 1), :]
            s = lax.dot_general(qh, kb, (((1,), (1,)), ((), ())), preferred_element_type=F32)
            s = s + cq - ck
            if masked:
                s = jnp.where(cpos <= rpos, s, NEG_INF)
            m_new = jnp.maximum(m, jnp.max(s, axis=-1, keepdims=True))
            alpha = jnp.exp(m - m_new)
            p = jnp.exp(s - m_new)
            l = alpha * l + jnp.sum(p, axis=-1, keepdims=True)
            acc = alpha * acc + jnp.dot(p.astype(BF16), vb, preferred_element_type=F32)
            return m_new, l, acc

        init = (jnp.full((tq, 1), NEG_INF, F32), jnp.zeros((tq, 1), F32),
                jnp.zeros((tq, LANES), F32))
        carry = lax.fori_loop(0, qi, functools.partial(step, masked=False), init)
        m, l, acc = step(qi, carry, True)
        outs.append(acc / l)
    o_ref[...] = jnp.where(lane < LANES // 2, outs[0], outs[1]).astype(BF16)


def _attention(q, k, v, cum, cum_rows):
    bsz, seq, d_b = q.shape
    tq = TQ
    n_kb = seq // tq
    return pl.pallas_call(
        _attn_kernel,
        grid=(bsz, d_b // LANES, seq // tq),
        in_specs=[pl.BlockSpec((None, tq, LANES), lambda b, p, i: (b, i, p)),
                  pl.BlockSpec((None, seq, LANES), lambda b, p, i: (b, 0, p)),
                  pl.BlockSpec((None, seq, LANES), lambda b, p, i: (b, 0, p)),
                  pl.BlockSpec((None, tq, LANES), lambda b, p, i: (b, i, 0)),
                  pl.BlockSpec((None, H_B, n_kb, tq), lambda b, p, i: (b, 0, 0, 0))],
        out_specs=pl.BlockSpec((None, tq, LANES), lambda b, p, i: (b, i, p)),
        out_shape=jax.ShapeDtypeStruct((bsz, seq, d_b), BF16),
        compiler_params=pltpu.CompilerParams(
            dimension_semantics=("arbitrary", "arbitrary", "arbitrary"),
            vmem_limit_bytes=VMEM_LIMIT_BYTES),
        name="attn",
    )(q, k, v, cum, cum_rows)


def _mixout_kernel(x_ref, ya_ref, yb_ref, mod_ref, gnb_ref, gpost_ref, wo_ref, o_ref):
    d_a = ya_ref.shape[1]
    yb = yb_ref[...].astype(F32)
    yb_n = (yb * _rms_scale(yb) * gnb_ref[...]).astype(BF16)
    y = (jnp.dot(ya_ref[...], wo_ref[:d_a, :], preferred_element_type=F32)
         + jnp.dot(yb_n, wo_ref[d_a:, :], preferred_element_type=F32))
    yn = y * _rms_scale(y) * gpost_ref[...]
    gate = mod_ref[2:3, :]
    o_ref[...] = x_ref[...] + (1.0 + gate) * yn


def _mix_out(x, ya, yb, mod, gnb, g_post, wo):
    bsz, seq, d = x.shape
    d_a = ya.shape[2]
    d_b = yb.shape[2]
    tm = TM_MIX
    tok = lambda width: pl.BlockSpec((None, tm, width), lambda b, s: (b, s, 0))
    return pl.pallas_call(
        _mixout_kernel,
        grid=(bsz, seq // tm),
        in_specs=[tok(d), tok(d_a), tok(d_b),
                  pl.BlockSpec((None, None, 3, d), lambda b, s: (b, 1, 0, 0)),
                  _const_spec((1, d_b)), _const_spec((1, d)), _const_spec(wo.shape)],
        out_specs=tok(d),
        out_shape=jax.ShapeDtypeStruct(x.shape, F32),
        compiler_params=pltpu.CompilerParams(
            dimension_semantics=("arbitrary", "arbitrary"),
            vmem_limit_bytes=VMEM_LIMIT_BYTES),
        name="mix_out",
    )(x, ya, yb, mod, gnb.reshape(1, d_b), g_post.reshape(1, d), wo)


def kernel(x, c, w_ada, b_ada, norm_pre_g, norm_post_g, ffn_w_gate, ffn_w_up, ffn_w_down, w_in,
           sgu_ln_g, sgu_ln_b, sgu_w, sgu_b, fox_b_f, gnorm_a_g, gnorm_b_g, w_out):
    bsz, seq, d = x.shape
    depth = w_ada.shape[0]
    d_a = sgu_ln_g.shape[1]
    d_b = gnorm_b_g.shape[1]
    dh_a = d_a // H_A
    for l in range(depth):
        mod = _adaln(c, w_ada[l], b_ada[l]).reshape(bsz, N_SUB, 3, d)

        x = _ffn(x, mod, 0, norm_pre_g[l, 0], norm_post_g[l, 0],
                 ffn_w_gate[l, 0].astype(BF16), ffn_w_up[l, 0].astype(BF16),
                 ffn_w_down[l, 0].astype(BF16), 0.5)

        w_l = w_in[l]
        wz = w_l[:, :2 * d_a].astype(BF16)
        wqkv = w_l[:, 2 * d_a:2 * d_a + 3 * d_b].astype(BF16)
        wf = jnp.pad(w_l[:, 2 * d_a + 3 * d_b:], ((0, 0), (0, LANES - H_B))).astype(BF16)
        bf = jnp.pad(fox_b_f[l], (0, LANES - H_B)).reshape(1, LANES)
        sgu_bias = jnp.repeat(sgu_b[l].T, dh_a, axis=1)
        ya, q, k, v, cum = _mix_in(x, mod, norm_pre_g[l, 1], wz, wqkv, wf, bf,
                                   sgu_ln_g[l], sgu_ln_b[l], sgu_w[l], sgu_bias, gnorm_a_g[l])
        cum_rows = cum[:, :, :H_B].transpose(0, 2, 1).reshape(bsz, H_B, seq // TQ, TQ)
        yb = _attention(q, k, v, cum, cum_rows)
        x = _mix_out(x, ya, yb, mod, gnorm_b_g[l], norm_post_g[l, 1], w_out[l].astype(BF16))

        x = _ffn(x, mod, 2, norm_pre_g[l, 2], norm_post_g[l, 2],
                 ffn_w_gate[l, 1].astype(BF16), ffn_w_up[l, 1].astype(BF16),
                 ffn_w_down[l, 1].astype(BF16), 0.5)
    return x
```

```python
import functools
import math

import jax
import jax.numpy as jnp
import numpy as np
from jax import lax
from jax.experimental import pallas as pl
from jax.experimental.pallas import tpu as pltpu

F32 = jnp.float32
BF16 = jnp.bfloat16

CHUNK = 64
H_A = 8
H_B = 8
SGU_BLOCK = 128
N_SUB = 3
EPS = 1e-6
NEG_INF = -1e30
LOG2E = math.log2(math.e)

LANES = 128
BF16_SUBLANES = 16
VMEM_LIMIT_BYTES = 56 * 1024 * 1024

TM_FFN = 512
FF_CHUNK = 256
TM_MIX = 512
T_ATT = 512
N_BIAS = 3
V_ROWS = 80


def _const_spec(shape):
    nd = len(shape)
    return pl.BlockSpec(shape, lambda *_: (0,) * nd, pipeline_mode=pl.Buffered(1))


def _rms_scale(v):
    return lax.rsqrt(jnp.mean(v * v, axis=-1, keepdims=True) + EPS)


def _modulated(x, mod_ref, gpre_ref):
    shift = mod_ref[0:1, :]
    scale = mod_ref[1:2, :]
    pre = gpre_ref[...] * (1.0 + scale)
    return (x * _rms_scale(x) * pre + shift).astype(BF16)


def _silu(v):
    return v * jax.nn.sigmoid(v)


def _gelu_tanh(v):
    c = math.sqrt(2.0 / math.pi)
    return 0.5 * v * (1.0 + jnp.tanh(c * (v + 0.044715 * (v * v * v))))


def _log_sigmoid(v):
    return jnp.minimum(v, 0.0) - jnp.log(1.0 + jnp.exp(-jnp.abs(v)))


def _split3(v):
    hi = v.astype(BF16).astype(F32)
    r1 = v - hi
    mid = r1.astype(BF16).astype(F32)
    lo = r1 - mid
    return hi, mid, lo


def _dot_nt(a, b):
    return lax.dot_general(a, b, (((1,), (1,)), ((), ())), preferred_element_type=F32)


def _adaln_kernel(c_ref, w_ref, b_ref, o_ref):
    sc = _silu(c_ref[...])
    o_ref[...] = jnp.dot(sc, w_ref[...], preferred_element_type=F32,
                         precision=lax.Precision.HIGHEST) + b_ref[...]


def _adaln(c, w, b):
    bsz, d = c.shape
    n = w.shape[1]
    tn = 1024
    return pl.pallas_call(
        _adaln_kernel,
        grid=(n // tn,),
        in_specs=[pl.BlockSpec((bsz, d), lambda j: (0, 0)),
                  pl.BlockSpec((d, tn), lambda j: (0, j)),
                  pl.BlockSpec((1, tn), lambda j: (0, j))],
        out_specs=pl.BlockSpec((bsz, tn), lambda j: (0, j)),
        out_shape=jax.ShapeDtypeStruct((bsz, n), F32),
        compiler_params=pltpu.CompilerParams(dimension_semantics=("arbitrary",)),
        name="adaln",
    )(c, w, b.reshape(1, n))


def _ffn_kernel(x_ref, mod_ref, gpre_ref, gpost_ref, wg_ref, wu_ref, wd_ref, o_ref, acc_ref,
                *, res_scale):
    x = x_ref[...]
    hb = _modulated(x, mod_ref, gpre_ref)
    d_ff = wg_ref.shape[1]
    for ci in range(d_ff // FF_CHUNK):
        cs = slice(ci * FF_CHUNK, (ci + 1) * FF_CHUNK)
        g = jnp.dot(hb, wg_ref[:, cs], preferred_element_type=F32)
        u = jnp.dot(hb, wu_ref[:, cs], preferred_element_type=F32)
        a = (_silu(g) * u).astype(BF16)
        part = jnp.dot(a, wd_ref[cs, :], preferred_element_type=F32)
        if ci == 0:
            acc_ref[...] = part
        else:
            acc_ref[...] += part
    y = acc_ref[...]
    yn = y * _rms_scale(y) * gpost_ref[...]
    gate = mod_ref[2:3, :]
    o_ref[...] = x + (res_scale * (1.0 + gate)) * yn


def _ffn(x, mod, sub, g_pre, g_post, wg, wu, wd, res_scale):
    bsz, seq, d = x.shape
    d_ff = wg.shape[1]
    tm = TM_FFN
    tok = pl.BlockSpec((None, tm, d), lambda b, s: (b, s, 0))
    return pl.pallas_call(
        functools.partial(_ffn_kernel, res_scale=res_scale),
        grid=(bsz, seq // tm),
        in_specs=[tok,
                  pl.BlockSpec((None, None, 3, d), lambda b, s: (b, sub, 0, 0)),
                  _const_spec((1, d)), _const_spec((1, d)),
                  _const_spec((d, d_ff)), _const_spec((d, d_ff)), _const_spec((d_ff, d))],
        out_specs=tok,
        out_shape=jax.ShapeDtypeStruct(x.shape, F32),
        scratch_shapes=[pltpu.VMEM((tm, d), F32)],
        compiler_params=pltpu.CompilerParams(
            dimension_semantics=("arbitrary", "arbitrary"),
            vmem_limit_bytes=VMEM_LIMIT_BYTES),
        name="ffn",
    )(x, mod, g_pre.reshape(1, d), g_post.reshape(1, d), wg, wu, wd)


def _mixin_kernel(x_ref, mod_ref, gpre_ref, wz_ref, wqt_ref, wk_ref, wvt_ref, wf_ref, bf_ref,
                  place_ref, kones_ref, lng_ref, lnb_ref, sguw_ref, sgub_ref, gna_ref,
                  ya_ref, qt_ref, ka_ref, vt_ref,
                  gate_ref, carry_ref, *, q_scale):
    tm = x_ref.shape[0]
    d_a = ya_ref.shape[1]
    n_heads = qt_ref.shape[0]
    dh = wqt_ref.shape[0] // n_heads
    hb = _modulated(x_ref[...], mod_ref, gpre_ref)

    z = _gelu_tanh(jnp.dot(hb, wz_ref[...], preferred_element_type=F32))
    u = z[:, :d_a]
    v = z[:, d_a:]
    vc = v - jnp.mean(v, axis=-1, keepdims=True)
    vn = (vc * lax.rsqrt(jnp.mean(vc * vc, axis=-1, keepdims=True) + EPS) * lng_ref[...]
          + lnb_ref[...]).astype(BF16)

    row = lax.broadcasted_iota(jnp.int32, (SGU_BLOCK, SGU_BLOCK), 0)
    col = lax.broadcasted_iota(jnp.int32, (SGU_BLOCK, SGU_BLOCK), 1)
    chunk_causal = (col // CHUNK) <= (row // CHUNK)
    low_half = col < (LANES // 2)
    zero = jnp.zeros((), BF16)
    for p in range(d_a // LANES):
        w_even = jnp.where(chunk_causal, sguw_ref[2 * p], 0.0).astype(BF16)
        w_odd = jnp.where(chunk_causal, sguw_ref[2 * p + 1], 0.0).astype(BF16)
        w_cat = jnp.concatenate([w_even, w_odd], axis=1)
        for blk in range(tm // SGU_BLOCK):
            rows = slice(blk * SGU_BLOCK, (blk + 1) * SGU_BLOCK)
            lanes = slice(p * LANES, (p + 1) * LANES)
            vt = vn[rows, lanes]
            rhs = jnp.concatenate([jnp.where(low_half, vt, zero),
                                   jnp.where(low_half, zero, vt)], axis=0)
            gate_ref[rows, lanes] = (jnp.dot(w_cat, rhs, preferred_element_type=F32)
                                     + sgub_ref[:, lanes])
    ya = u * gate_ref[...]
    ya_ref[...] = (ya * _rms_scale(ya) * gna_ref[...]).astype(BF16)

    @pl.when(pl.program_id(1) == 0)
    def _():
        carry_ref[...] = jnp.zeros_like(carry_ref)

    log_f = _log_sigmoid(jnp.dot(hb, wf_ref[...], preferred_element_type=F32) + bf_ref[...])
    r = lax.broadcasted_iota(jnp.int32, (tm, tm), 0)
    cidx = lax.broadcasted_iota(jnp.int32, (tm, tm), 1)
    tri = jnp.where(cidx <= r, 1.0, 0.0).astype(F32)
    cum = jnp.dot(tri, log_f, preferred_element_type=F32,
                  precision=lax.Precision.HIGHEST) + carry_ref[0:1, :]
    carry_ref[0:1, :] = cum[tm - 1:tm, :]
    cum2 = cum * LOG2E
    cum2_t = cum2.T[0:8, :]

    hi, mid, lo = _split3(cum2)
    parts = jnp.concatenate([hi, mid, lo], axis=1).astype(BF16)
    ka = (jnp.dot(hb, wk_ref[...], preferred_element_type=F32)
          + jnp.dot(parts, place_ref[...], preferred_element_type=F32)
          + kones_ref[...])
    ka_ref[...] = ka.astype(BF16)

    qt = _dot_nt(wqt_ref[...], hb) * q_scale
    vt_all = _dot_nt(wvt_ref[...], hb)
    hi_t, mid_t, lo_t = _split3(cum2_t)
    rid = lax.broadcasted_iota(jnp.int32, (BF16_SUBLANES, tm), 0)
    ones_row = jnp.where(rid == 0, 1.0, 0.0).astype(BF16)
    pad_rows = qt_ref.shape[1] - dh - BF16_SUBLANES
    for h in range(n_heads):
        qt_ref[h, 0:dh, :] = qt[h * dh:(h + 1) * dh, :].astype(BF16)
        bias = jnp.where(rid == 0, hi_t[h:h + 1, :],
                         jnp.where(rid == 1, mid_t[h:h + 1, :],
                                   jnp.where(rid == 2, lo_t[h:h + 1, :],
                                             jnp.where(rid < 2 * N_BIAS, 1.0, 0.0))))
        qt_ref[h, dh:dh + BF16_SUBLANES, :] = bias.astype(BF16)
        qt_ref[h, dh + BF16_SUBLANES:, :] = jnp.zeros((pad_rows, tm), BF16)
        vt_ref[h, 0:dh, :] = vt_all[h * dh:(h + 1) * dh, :].astype(BF16)
        vt_ref[h, dh:, :] = ones_row


def _mix_in(x, mod, g_pre, wz, wqt, wk, wvt, wf, bf, place, kones, ln_g, ln_b, sgu_w, sgu_bias,
            gna):
    bsz, seq, d = x.shape
    d_a = wz.shape[1] // 2
    d_b = wqt.shape[0]
    dh = d_b // H_B
    tm = TM_MIX
    tok = lambda width: pl.BlockSpec((None, tm, width), lambda b, s: (b, s, 0))
    return pl.pallas_call(
        functools.partial(_mixin_kernel, q_scale=dh ** -0.5 * LOG2E),
        grid=(bsz, seq // tm),
        in_specs=[tok(d),
                  pl.BlockSpec((None, None, 3, d), lambda b, s: (b, 1, 0, 0)),
                  _const_spec((1, d)),
                  _const_spec(wz.shape), _const_spec(wqt.shape), _const_spec(wk.shape),
                  _const_spec(wvt.shape), _const_spec(wf.shape), _const_spec((1, LANES)),
                  _const_spec(place.shape), _const_spec(kones.shape),
                  _const_spec((1, d_a)), _const_spec((1, d_a)),
                  _const_spec(sgu_w.shape), _const_spec(sgu_bias.shape),
                  _const_spec((1, d_a))],
        out_specs=[tok(d_a),
                   pl.BlockSpec((None, H_B, LANES, tm), lambda b, s: (b, 0, 0, s)),
                   tok(H_B * LANES),
                   pl.BlockSpec((None, H_B, None, V_ROWS, tm), lambda b, s: (b, 0, s, 0, 0))],
        out_shape=[jax.ShapeDtypeStruct((bsz, seq, d_a), BF16),
                   jax.ShapeDtypeStruct((bsz, H_B, LANES, seq), BF16),
                   jax.ShapeDtypeStruct((bsz, seq, H_B * LANES), BF16),
                   jax.ShapeDtypeStruct((bsz, H_B, seq // tm, V_ROWS, tm), BF16)],
        scratch_shapes=[pltpu.VMEM((tm, d_a), F32), pltpu.VMEM((8, LANES), F32)],
        compiler_params=pltpu.CompilerParams(
            dimension_semantics=("arbitrary", "arbitrary"),
            vmem_limit_bytes=VMEM_LIMIT_BYTES),
        name="mix_in",
    )(x, mod, g_pre.reshape(1, d), wz, wqt, wk, wvt, wf, bf, place, kones,
      ln_g.reshape(1, d_a), ln_b.reshape(1, d_a), sgu_w, sgu_bias, gna.reshape(1, d_a))


def _attn_kernel(qt0_ref, qt1_ref, ka0_ref, ka1_ref, vt0_ref, vt1_ref, o_ref,
                 s0_ref, s1_ref, mx_ref, m_ref, acc_ref):
    t = qt0_ref.shape[1]
    dh = o_ref.shape[0] // 2
    qi = pl.program_id(2)
    heads = ((qt0_ref, ka0_ref, vt0_ref), (qt1_ref, ka1_ref, vt1_ref))
    slots = (s0_ref, s1_ref)

    def logits(j, slot, diagonal):
        rows = pl.ds(pl.multiple_of(j * t, t), t)
        for i, (qt_ref, ka_ref, _) in enumerate(heads):
            s = jnp.dot(ka_ref[rows, :], qt_ref[...], preferred_element_type=F32)
            if diagonal:
                kpos = lax.broadcasted_iota(jnp.int32, (t, t), 0)
                qpos = lax.broadcasted_iota(jnp.int32, (t, t), 1)
                s = jnp.where(kpos <= qpos, s, NEG_INF)
            slots[slot][i] = s
            mx_ref[slot, i] = jnp.max(s, axis=0, keepdims=True)

    def consume(j, slot):
        for i, (_, _, vt_ref) in enumerate(heads):
            m_old = m_ref[i]
            m_new = jnp.maximum(m_old, mx_ref[slot, i])
            p = jnp.exp2(slots[slot][i] - m_new).astype(BF16)
            m_ref[i] = m_new
            acc_ref[i] = (jnp.exp2(m_old - m_new) * acc_ref[i]
                          + jnp.dot(vt_ref[j], p, preferred_element_type=F32))

    m_ref[...] = jnp.full(m_ref.shape, NEG_INF, F32)
    acc_ref[...] = jnp.zeros(acc_ref.shape, F32)

    @pl.when(qi == 0)
    def _():
        logits(0, 0, True)
        consume(0, 0)

    @pl.when(qi > 0)
    def _():
        logits(0, 0, False)

    n_pairs = jnp.maximum(qi - 1, 0) // 2

    def pair(jj, carry):
        j = 2 * jj
        logits(j + 1, 1, False)
        consume(j, 0)
        logits(j + 2, 0, False)
        consume(j + 1, 1)
        return carry

    lax.fori_loop(0, n_pairs, pair, 0)

    @pl.when(qi % 2 == 1)
    def _():
        logits(qi, 1, True)
        consume(qi - 1, 0)
        consume(qi, 1)

    @pl.when(jnp.logical_and(qi % 2 == 0, qi > 0))
    def _():
        logits(qi - 1, 1, False)
        consume(qi - 2, 0)
        logits(qi, 0, True)
        consume(qi - 1, 1)
        consume(qi, 0)

    for i in range(2):
        acc = acc_ref[i]
        o_ref[i * dh:(i + 1) * dh, :] = (acc[0:dh, :] / acc[dh:dh + 1, :]).astype(BF16)


def _attention(qt, ka, vt):
    bsz, n_heads, _, seq = qt.shape
    t = T_ATT
    n_kb = seq // t
    dh = (V_ROWS - BF16_SUBLANES)
    qspec = lambda i: pl.BlockSpec((None, None, LANES, t), lambda b, p, q: (b, 2 * p + i, 0, q))
    kspec = lambda i: pl.BlockSpec((None, seq, LANES), lambda b, p, q: (b, 0, 2 * p + i))
    vspec = lambda i: pl.BlockSpec((None, None, n_kb, V_ROWS, t),
                                   lambda b, p, q: (b, 2 * p + i, 0, 0, 0))
    return pl.pallas_call(
        _attn_kernel,
        grid=(bsz, n_heads // 2, seq // t),
        in_specs=[qspec(0), qspec(1), kspec(0), kspec(1), vspec(0), vspec(1)],
        out_specs=pl.BlockSpec((None, 2 * dh, t), lambda b, p, q: (b, p, q)),
        out_shape=jax.ShapeDtypeStruct((bsz, n_heads * dh, seq), BF16),
        scratch_shapes=[pltpu.VMEM((2, t, t), F32), pltpu.VMEM((2, t, t), F32),
                        pltpu.VMEM((2, 2, 1, t), F32), pltpu.VMEM((2, 1, t), F32),
                        pltpu.VMEM((2, V_ROWS, t), F32)],
        compiler_params=pltpu.CompilerParams(
            dimension_semantics=("arbitrary", "arbitrary", "arbitrary"),
            vmem_limit_bytes=VMEM_LIMIT_BYTES),
        name="attn",
    )(qt, qt, ka, ka, vt, vt)


def _mixout_kernel(x_ref, ya_ref, ybt_ref, mod_ref, gnb_ref, gpost_ref, wo_ref, o_ref):
    d_a = ya_ref.shape[1]
    tm = x_ref.shape[0]
    ybt = ybt_ref[...].astype(F32)
    r = lax.rsqrt(jnp.mean(ybt * ybt, axis=0, keepdims=True) + EPS)
    gnb = jnp.concatenate([gnb_ref[...]] * (tm // LANES), axis=1)
    yb_n = (ybt * r * gnb).T.astype(BF16)
    y = (jnp.dot(ya_ref[...], wo_ref[:d_a, :], preferred_element_type=F32)
         + jnp.dot(yb_n, wo_ref[d_a:, :], preferred_element_type=F32))
    yn = y * _rms_scale(y) * gpost_ref[...]
    gate = mod_ref[2:3, :]
    o_ref[...] = x_ref[...] + (1.0 + gate) * yn


def _mix_out(x, ya, ybt, mod, gnb_cols, g_post, wo):
    bsz, seq, d = x.shape
    d_a = ya.shape[2]
    d_b = ybt.shape[1]
    tm = TM_MIX
    tok = lambda width: pl.BlockSpec((None, tm, width), lambda b, s: (b, s, 0))
    return pl.pallas_call(
        _mixout_kernel,
        grid=(bsz, seq // tm),
        in_specs=[tok(d), tok(d_a),
                  pl.BlockSpec((None, d_b, tm), lambda b, s: (b, 0, s)),
                  pl.BlockSpec((None, None, 3, d), lambda b, s: (b, 1, 0, 0)),
                  _const_spec(gnb_cols.shape), _const_spec((1, d)), _const_spec(wo.shape)],
        out_specs=tok(d),
        out_shape=jax.ShapeDtypeStruct(x.shape, F32),
        compiler_params=pltpu.CompilerParams(
            dimension_semantics=("arbitrary", "arbitrary"),
            vmem_limit_bytes=VMEM_LIMIT_BYTES),
        name="mix_out",
    )(x, ya, ybt, mod, gnb_cols, g_post.reshape(1, d), wo)


def _key_bias_constants(n_heads, dh):
    place = np.zeros((N_BIAS * LANES, n_heads * LANES), np.float32)
    ones = np.zeros((1, n_heads * LANES), np.float32)
    for h in range(n_heads):
        for t in range(N_BIAS):
            place[t * LANES + h, h * LANES + dh + N_BIAS + t] = -1.0
            ones[0, h * LANES + dh + t] = 1.0
    return jnp.asarray(place, BF16), jnp.asarray(ones, F32)


def kernel(x, c, w_ada, b_ada, norm_pre_g, norm_post_g, ffn_w_gate, ffn_w_up, ffn_w_down, w_in,
           sgu_ln_g, sgu_ln_b, sgu_w, sgu_b, fox_b_f, gnorm_a_g, gnorm_b_g, w_out):
    bsz, seq, d = x.shape
    depth = w_ada.shape[0]
    d_a = sgu_ln_g.shape[1]
    d_b = gnorm_b_g.shape[1]
    dh_a = d_a // H_A
    dh = d_b // H_B
    assert dh + BF16_SUBLANES == V_ROWS and TM_MIX == T_ATT
    place, kones = _key_bias_constants(H_B, dh)
    for l in range(depth):
        mod = _adaln(c, w_ada[l], b_ada[l]).reshape(bsz, N_SUB, 3, d)

        x = _ffn(x, mod, 0, norm_pre_g[l, 0], norm_post_g[l, 0],
                 ffn_w_gate[l, 0].astype(BF16), ffn_w_up[l, 0].astype(BF16),
                 ffn_w_down[l, 0].astype(BF16), 0.5)

        w_l = w_in[l]
        off_q = 2 * d_a
        wz = w_l[:, :off_q].astype(BF16)
        wqt = w_l[:, off_q:off_q + d_b].T.astype(BF16)
        wk = jnp.pad(w_l[:, off_q + d_b:off_q + 2 * d_b].reshape(d, H_B, dh),
                     ((0, 0), (0, 0), (0, LANES - dh))).reshape(d, H_B * LANES).astype(BF16)
        wvt = w_l[:, off_q + 2 * d_b:off_q + 3 * d_b].T.astype(BF16)
        wf = jnp.pad(w_l[:, off_q + 3 * d_b:], ((0, 0), (0, LANES - H_B))).astype(BF16)
        bf = jnp.pad(fox_b_f[l], (0, LANES - H_B)).reshape(1, LANES)
        sgu_bias = jnp.repeat(sgu_b[l].T, dh_a, axis=1)
        ya, qt, ka, vt = _mix_in(x, mod, norm_pre_g[l, 1], wz, wqt, wk, wvt, wf, bf, place, kones,
                                 sgu_ln_g[l], sgu_ln_b[l], sgu_w[l], sgu_bias, gnorm_a_g[l])
        ybt = _attention(qt, ka, vt)
        gnb_cols = jnp.broadcast_to(gnorm_b_g[l][:, None], (d_b, LANES))
        x = _mix_out(x, ya, ybt, mod, gnb_cols, norm_post_g[l, 1], w_out[l].astype(BF16))

        x = _ffn(x, mod, 2, norm_pre_g[l, 2], norm_post_g[l, 2],
                 ffn_w_gate[l, 1].astype(BF16), ffn_w_up[l, 1].astype(BF16),
                 ffn_w_down[l, 1].astype(BF16), 0.5)
    return x
```

```python
import functools
import math

import jax
import jax.numpy as jnp
import numpy as np
from jax import lax
from jax.experimental import pallas as pl
from jax.experimental.pallas import tpu as pltpu

F32 = jnp.float32
BF16 = jnp.bfloat16

CHUNK = 64
H_A = 8
H_B = 8
SGU_BLOCK = 128
N_SUB = 3
EPS = 1e-6
NEG_INF = -1e30
LOG2E = math.log2(math.e)

LANES = 128
SUBLANES = 8
BF16_SUBLANES = 16
VMEM_LIMIT_BYTES = 56 * 1024 * 1024

TM_FFN = 1024
FF_CHUNK = 256
HALF_OVERLAP = 3
TM_MIX = 512
T_ATT = 512
QCOLS = 256
N_BIAS = 3
BIAS_SLOTS = 2 * N_BIAS
V_ROWS = 80


def _const_spec(shape):
    nd = len(shape)
    return pl.BlockSpec(shape, lambda *_: (0,) * nd, pipeline_mode=pl.Buffered(1))


def _rms_scale(v):
    return lax.rsqrt(jnp.mean(v * v, axis=-1, keepdims=True) + EPS)


def _modulated(x, mod_ref, gpre_ref):
    shift = mod_ref[0:1, :]
    scale = mod_ref[1:2, :]
    pre = gpre_ref[...] * (1.0 + scale)
    return (x * _rms_scale(x) * pre + shift).astype(BF16)


def _silu(v):
    return v * jax.nn.sigmoid(v)


def _gelu_tanh(v):
    c = math.sqrt(2.0 / math.pi)
    return 0.5 * v * (1.0 + jnp.tanh(c * (v + 0.044715 * (v * v * v))))


def _log_sigmoid(v):
    return jnp.minimum(v, 0.0) - jnp.log(1.0 + jnp.exp(-jnp.abs(v)))


def _split3(v):
    hi = v.astype(BF16).astype(F32)
    r1 = v - hi
    mid = r1.astype(BF16).astype(F32)
    lo = r1 - mid
    return hi, mid, lo


def _dot_nt(a, b):
    return lax.dot_general(a, b, (((1,), (1,)), ((), ())), preferred_element_type=F32)


def _cumsum_lanes(v):
    n = v.shape[1]
    lane = lax.broadcasted_iota(jnp.int32, v.shape, 1)
    shift = 1
    while shift < n:
        v = v + jnp.where(lane >= shift, pltpu.roll(v, shift, axis=1), 0.0)
        shift *= 2
    return v


def _adaln_kernel(c_ref, w_ref, b_ref, o_ref):
    sc = _silu(c_ref[...])
    o_ref[...] = jnp.dot(sc, w_ref[...], preferred_element_type=F32,
                         precision=lax.Precision.HIGHEST) + b_ref[...]


def _adaln(c, w, b):
    bsz, d = c.shape
    n = w.shape[1]
    tn = 1024
    return pl.pallas_call(
        _adaln_kernel,
        grid=(n // tn,),
        in_specs=[pl.BlockSpec((bsz, d), lambda j: (0, 0)),
                  pl.BlockSpec((d, tn), lambda j: (0, j)),
                  pl.BlockSpec((1, tn), lambda j: (0, j))],
        out_specs=pl.BlockSpec((bsz, tn), lambda j: (0, j)),
        out_shape=jax.ShapeDtypeStruct((bsz, n), F32),
        compiler_params=pltpu.CompilerParams(dimension_semantics=("arbitrary",)),
        name="adaln",
    )(c, w, b.reshape(1, n))


def _ffn_halves(x_ref, mod_ref, gpre_ref, gpost_ref, wg_ref, wu_ref, wd_ref, o_ref, acc_ref,
                res_scale, before_half=lambda h: None):
    half = x_ref.shape[0] // 2
    n_chunks = wg_ref.shape[1] // FF_CHUNK
    rows = (slice(0, half), slice(half, 2 * half))

    def chunk(hb, h, ci):
        cs = slice(ci * FF_CHUNK, (ci + 1) * FF_CHUNK)
        g = jnp.dot(hb, wg_ref[:, cs], preferred_element_type=F32)
        u = jnp.dot(hb, wu_ref[:, cs], preferred_element_type=F32)
        a = (_silu(g) * u).astype(BF16)
        part = jnp.dot(a, wd_ref[cs, :], preferred_element_type=F32)
        if ci == 0:
            acc_ref[h] = part
        else:
            acc_ref[h] += part

    def finish(h):
        y = acc_ref[h]
        yn = y * _rms_scale(y) * gpost_ref[...]
        gate = mod_ref[2:3, :]
        o_ref[rows[h], :] = x_ref[rows[h], :] + (res_scale * (1.0 + gate)) * yn

    before_half(0)
    hb0 = _modulated(x_ref[rows[0], :], mod_ref, gpre_ref)
    solo = n_chunks - HALF_OVERLAP
    for ci in range(solo):
        chunk(hb0, 0, ci)
    before_half(1)
    hb1 = _modulated(x_ref[rows[1], :], mod_ref, gpre_ref)
    for k in range(HALF_OVERLAP):
        chunk(hb0, 0, solo + k)
        chunk(hb1, 1, k)
    finish(0)
    for ci in range(HALF_OVERLAP, n_chunks):
        chunk(hb1, 1, ci)
    finish(1)


def _ffn_kernel(x_ref, mod_ref, gpre_ref, gpost_ref, wg_ref, wu_ref, wd_ref, o_ref, acc_ref,
                *, res_scale):
    _ffn_halves(x_ref, mod_ref, gpre_ref, gpost_ref, wg_ref, wu_ref, wd_ref, o_ref, acc_ref,
                res_scale)


def _ffn(x, mod, sub, g_pre, g_post, wg, wu, wd, res_scale):
    bsz, seq, d = x.shape
    d_ff = wg.shape[1]
    tm = TM_FFN
    tok = pl.BlockSpec((None, tm, d), lambda b, s: (b, s, 0))
    return pl.pallas_call(
        functools.partial(_ffn_kernel, res_scale=res_scale),
        grid=(bsz, seq // tm),
        in_specs=[tok,
                  pl.BlockSpec((None, None, 3, d), lambda b, s: (b, sub, 0, 0)),
                  _const_spec((1, d)), _const_spec((1, d)),
                  _const_spec((d, d_ff)), _const_spec((d, d_ff)), _const_spec((d_ff, d))],
        out_specs=tok,
        out_shape=jax.ShapeDtypeStruct(x.shape, F32),
        scratch_shapes=[pltpu.VMEM((2, tm // 2, d), F32)],
        compiler_params=pltpu.CompilerParams(
            dimension_semantics=("arbitrary", "arbitrary"),
            vmem_limit_bytes=VMEM_LIMIT_BYTES),
        name="ffn",
    )(x, mod, g_pre.reshape(1, d), g_post.reshape(1, d), wg, wu, wd)


def _mixin_kernel(x_ref, mod_ref, gpre_ref, wz_ref, wqvf_ref, wk_ref, bf_ref,
                  place_ref, kones_ref, lng_ref, lnb_ref, sguw_ref, sgub_ref, gna_ref,
                  ya_ref, qt_ref, qb_ref, ka_ref, vt_ref,
                  gate_ref, carry_ref):
    tm = x_ref.shape[0]
    d_a = ya_ref.shape[1]
    d_b = qt_ref.shape[0]
    n_heads = qb_ref.shape[0]
    dh = d_b // n_heads
    hb = _modulated(x_ref[...], mod_ref, gpre_ref)

    n_zc = 2 * d_a // FF_CHUNK
    z_cols = lambda ci: slice(ci * FF_CHUNK, (ci + 1) * FF_CHUNK)
    z_chunks = []
    pre = jnp.dot(hb, wz_ref[:, z_cols(0)], preferred_element_type=F32)
    for ci in range(1, n_zc):
        nxt = jnp.dot(hb, wz_ref[:, z_cols(ci)], preferred_element_type=F32)
        z_chunks.append(_gelu_tanh(pre))
        pre = nxt
    q_t = _dot_nt(wqvf_ref[0:d_b, :], hb)
    z_chunks.append(_gelu_tanh(pre))
    u = jnp.concatenate(z_chunks[:n_zc // 2], axis=1)
    v = jnp.concatenate(z_chunks[n_zc // 2:], axis=1)

    vf_t = _dot_nt(wqvf_ref[d_b:, :], hb)
    qt_ref[...] = (q_t * (dh ** -0.5 * LOG2E)).astype(BF16)
    vc = v - jnp.mean(v, axis=-1, keepdims=True)
    vn = (vc * lax.rsqrt(jnp.mean(vc * vc, axis=-1, keepdims=True) + EPS) * lng_ref[...]
          + lnb_ref[...]).astype(BF16)
    kc = jnp.dot(hb, wk_ref[...], preferred_element_type=F32)
    rid = lax.broadcasted_iota(jnp.int32, (BF16_SUBLANES, tm), 0)
    ones_row = jnp.where(rid == 0, 1.0, 0.0).astype(BF16)
    for h in range(n_heads):
        vt_ref[h, 0:dh, :] = vf_t[h * dh:(h + 1) * dh, :].astype(BF16)
        vt_ref[h, dh:, :] = ones_row

    @pl.when(pl.program_id(1) == 0)
    def _():
        carry_ref[...] = jnp.zeros_like(carry_ref)

    log_f = _log_sigmoid(vf_t[d_b:d_b + SUBLANES, :] + bf_ref[...])
    cum = _cumsum_lanes(log_f) + carry_ref[...]
    carry_ref[...] = jnp.broadcast_to(cum[:, tm - 1:tm], cum.shape)
    hi, mid, lo = _split3(cum * LOG2E)

    for h in range(n_heads):
        off = BIAS_SLOTS * (h % 2)
        inside = jnp.logical_and(rid >= off, rid < off + BIAS_SLOTS)
        bias = jnp.where(rid == off, hi[h:h + 1, :],
                         jnp.where(rid == off + 1, mid[h:h + 1, :],
                                   jnp.where(rid == off + 2, lo[h:h + 1, :],
                                             jnp.where(inside, 1.0, 0.0))))
        qb_ref[h] = bias.astype(BF16)

    parts_t = jnp.concatenate(
        [hi, mid, lo, jnp.zeros((LANES - N_BIAS * SUBLANES, tm), F32)], axis=0)
    parts = parts_t.T.astype(BF16)
    kb = jnp.dot(parts, place_ref[...], preferred_element_type=F32) + kones_ref[...]
    for p in range(n_heads // 2):
        lanes = slice(p * LANES, (p + 1) * LANES)
        ka_ref[:, 2 * p * LANES:(2 * p + 1) * LANES] = kc[:, lanes].astype(BF16)
        ka_ref[:, (2 * p + 1) * LANES:(2 * p + 2) * LANES] = kb[:, lanes].astype(BF16)

    row = lax.broadcasted_iota(jnp.int32, (SGU_BLOCK, SGU_BLOCK), 0)
    col = lax.broadcasted_iota(jnp.int32, (SGU_BLOCK, SGU_BLOCK), 1)
    chunk_causal = (col // CHUNK) <= (row // CHUNK)
    low_half = col < (LANES // 2)
    zero = jnp.zeros((), BF16)

    def head_split(vt):
        return jnp.concatenate([jnp.where(low_half, vt, zero), jnp.where(low_half, zero, vt)],
                               axis=0)

    for p in range(d_a // LANES):
        lanes = slice(p * LANES, (p + 1) * LANES)
        w_even = jnp.where(chunk_causal, sguw_ref[2 * p], 0.0).astype(BF16)
        w_odd = jnp.where(chunk_causal, sguw_ref[2 * p + 1], 0.0).astype(BF16)
        w_cat = jnp.concatenate([w_even, w_odd], axis=1)
        for bp in range(tm // (2 * SGU_BLOCK)):
            rows0 = slice(2 * bp * SGU_BLOCK, (2 * bp + 1) * SGU_BLOCK)
            rows1 = slice((2 * bp + 1) * SGU_BLOCK, (2 * bp + 2) * SGU_BLOCK)
            rhs = jnp.concatenate([head_split(vn[rows0, lanes]), head_split(vn[rows1, lanes])],
                                  axis=1)
            g2 = jnp.dot(w_cat, rhs, preferred_element_type=F32)
            gate_ref[rows0, lanes] = g2[:, :LANES] + sgub_ref[:, lanes]
            gate_ref[rows1, lanes] = g2[:, LANES:] + sgub_ref[:, lanes]
    ya = u * gate_ref[...]
    ya_ref[...] = (ya * _rms_scale(ya) * gna_ref[...]).astype(BF16)


def _mix_in(x, mod, g_pre, wz, wqvf, wk, bf_rows, place, kones, ln_g, ln_b, sgu_w, sgu_bias, gna):
    bsz, seq, d = x.shape
    d_a = wz.shape[1] // 2
    d_b = wk.shape[1]
    tm = TM_MIX
    tok = lambda width: pl.BlockSpec((None, tm, width), lambda b, s: (b, s, 0))
    return pl.pallas_call(
        _mixin_kernel,
        grid=(bsz, seq // tm),
        in_specs=[tok(d),
                  pl.BlockSpec((None, None, 3, d), lambda b, s: (b, 1, 0, 0)),
                  _const_spec((1, d)),
                  _const_spec(wz.shape), _const_spec(wqvf.shape), _const_spec(wk.shape),
                  _const_spec(bf_rows.shape), _const_spec(place.shape), _const_spec(kones.shape),
                  _const_spec((1, d_a)), _const_spec((1, d_a)),
                  _const_spec(sgu_w.shape), _const_spec(sgu_bias.shape),
                  _const_spec((1, d_a))],
        out_specs=[tok(d_a),
                   pl.BlockSpec((None, d_b, tm), lambda b, s: (b, 0, s)),
                   pl.BlockSpec((None, H_B, BF16_SUBLANES, tm), lambda b, s: (b, 0, 0, s)),
                   tok(2 * d_b),
                   pl.BlockSpec((None, H_B, None, V_ROWS, tm), lambda b, s: (b, 0, s, 0, 0))],
        out_shape=[jax.ShapeDtypeStruct((bsz, seq, d_a), BF16),
                   jax.ShapeDtypeStruct((bsz, d_b, seq), BF16),
                   jax.ShapeDtypeStruct((bsz, H_B, BF16_SUBLANES, seq), BF16),
                   jax.ShapeDtypeStruct((bsz, seq, 2 * d_b), BF16),
                   jax.ShapeDtypeStruct((bsz, H_B, seq // tm, V_ROWS, tm), BF16)],
        scratch_shapes=[pltpu.VMEM((tm, d_a), F32), pltpu.VMEM((SUBLANES, tm), F32)],
        compiler_params=pltpu.CompilerParams(
            dimension_semantics=("arbitrary", "arbitrary"),
            vmem_limit_bytes=VMEM_LIMIT_BYTES),
        name="mix_in",
    )(x, mod, g_pre.reshape(1, d), wz, wqvf, wk, bf_rows, place, kones,
      ln_g.reshape(1, d_a), ln_b.reshape(1, d_a), sgu_w, sgu_bias, gna.reshape(1, d_a))


def _attn_kernel(qt_ref, qb0_ref, qb1_ref, ka_ref, vt0_ref, vt1_ref, o_ref,
                 qa_ref, s0_ref, s1_ref, mx0_ref, mx1_ref, p0_ref, p1_ref, al0_ref, al1_ref,
                 m_ref, acc_ref):
    t = qt_ref.shape[1]
    dh = o_ref.shape[0] // 2
    qi = pl.program_id(2)
    vts = (vt0_ref, vt1_ref)
    slot_logits = (s0_ref, s1_ref)
    slot_max = (mx0_ref, mx1_ref)
    slot_p = (p0_ref, p1_ref)
    slot_alpha = (al0_ref, al1_ref)

    q_pair = qt_ref[...]
    zq = jnp.zeros((dh, t), BF16)
    zpad = jnp.zeros((2 * LANES - 2 * dh - BF16_SUBLANES, t), BF16)
    qa_ref[0] = jnp.concatenate([q_pair[0:dh], zq, qb0_ref[...], zpad], axis=0)
    qa_ref[1] = jnp.concatenate([zq, q_pair[dh:2 * dh], qb1_ref[...], zpad], axis=0)

    parts = [(i, slice(c * QCOLS, (c + 1) * QCOLS)) for i in range(2) for c in range(t // QCOLS)]

    def a_part(j, slot, diagonal, i, cols):
        kblk = ka_ref[pl.ds(pl.multiple_of(j * t, t), t), :]
        s = jnp.dot(kblk, qa_ref[i, :, cols], preferred_element_type=F32)
        if diagonal:
            kpos = lax.broadcasted_iota(jnp.int32, s.shape, 0)
            qpos = lax.broadcasted_iota(jnp.int32, s.shape, 1) + cols.start
            s = jnp.where(kpos <= qpos, s, NEG_INF)
        slot_logits[slot][i, :, cols] = s
        slot_max[slot][i, :, cols] = jnp.max(s, axis=0, keepdims=True)

    def b_part(slot, i, cols):
        m_old = m_ref[i, :, cols]
        m_new = jnp.maximum(m_old, slot_max[slot][i, :, cols])
        slot_alpha[slot][i, :, cols] = jnp.exp2(m_old - m_new)
        slot_p[slot][i, :, cols] = jnp.exp2(slot_logits[slot][i, :, cols] - m_new).astype(BF16)
        m_ref[i, :, cols] = m_new

    def c_part(j, slot, i, cols):
        acc_ref[i, :, cols] = (slot_alpha[slot][i, :, cols] * acc_ref[i, :, cols]
                               + jnp.dot(vts[i][j], slot_p[slot][i, :, cols],
                                         preferred_element_type=F32))

    def stage_a(j, slot, diagonal):
        for i, cols in parts:
            a_part(j, slot, diagonal, i, cols)

    def stage_b(slot):
        for i, cols in parts:
            b_part(slot, i, cols)

    def stage_c(j, slot):
        for i, cols in parts:
            c_part(j, slot, i, cols)

    def step(j, parity, next_diagonal):
        for i, cols in parts:
            if next_diagonal is not None:
                a_part(j + 1, 1 - parity, next_diagonal, i, cols)
            b_part(parity, i, cols)
            c_part(j, parity, i, cols)

    m_ref[...] = jnp.full(m_ref.shape, NEG_INF, F32)
    acc_ref[...] = jnp.zeros(acc_ref.shape, F32)

    @pl.when(qi == 0)
    def _():
        stage_a(0, 0, True)

    @pl.when(qi > 0)
    def _():
        stage_a(0, 0, False)

    n_pairs = jnp.maximum(qi - 1, 0) // 2

    def pair(jj, carry):
        j = 2 * jj
        step(j, 0, False)
        step(j + 1, 1, False)
        return carry

    lax.fori_loop(0, n_pairs, pair, 0)

    @pl.when(qi == 0)
    def _():
        step(0, 0, None)

    @pl.when(qi % 2 == 1)
    def _():
        step(qi - 1, 0, True)
        step(qi, 1, None)

    @pl.when(jnp.logical_and(qi % 2 == 0, qi > 0))
    def _():
        step(qi - 2, 0, False)
        step(qi - 1, 1, True)
        step(qi, 0, None)

    for i in range(2):
        acc = acc_ref[i]
        o_ref[i * dh:(i + 1) * dh, :] = (acc[0:dh, :] / acc[dh:dh + 1, :]).astype(BF16)


def _attention(qt, qb, ka, vt):
    bsz, d_b, seq = qt.shape
    n_heads = qb.shape[1]
    dh = d_b // n_heads
    t = T_ATT
    n_kb = seq // t
    qbspec = lambda i: pl.BlockSpec((None, None, BF16_SUBLANES, t),
                                    lambda b, p, q: (b, 2 * p + i, 0, q))
    vspec = lambda i: pl.BlockSpec((None, None, n_kb, V_ROWS, t),
                                   lambda b, p, q: (b, 2 * p + i, 0, 0, 0))
    return pl.pallas_call(
        _attn_kernel,
        grid=(bsz, n_heads // 2, seq // t),
        in_specs=[pl.BlockSpec((None, 2 * dh, t), lambda b, p, q: (b, p, q)),
                  qbspec(0), qbspec(1),
                  pl.BlockSpec((None, seq, 2 * LANES), lambda b, p, q: (b, 0, p)),
                  vspec(0), vspec(1)],
        out_specs=pl.BlockSpec((None, 2 * dh, t), lambda b, p, q: (b, p, q)),
        out_shape=jax.ShapeDtypeStruct((bsz, d_b, seq), BF16),
        scratch_shapes=[pltpu.VMEM((2, 2 * LANES, t), BF16),
                        pltpu.VMEM((2, t, t), F32), pltpu.VMEM((2, t, t), F32),
                        pltpu.VMEM((2, 1, t), F32), pltpu.VMEM((2, 1, t), F32),
                        pltpu.VMEM((2, t, t), BF16), pltpu.VMEM((2, t, t), BF16),
                        pltpu.VMEM((2, 1, t), F32), pltpu.VMEM((2, 1, t), F32),
                        pltpu.VMEM((2, 1, t), F32),
                        pltpu.VMEM((2, V_ROWS, t), F32)],
        compiler_params=pltpu.CompilerParams(
            dimension_semantics=("arbitrary", "arbitrary", "arbitrary"),
            vmem_limit_bytes=VMEM_LIMIT_BYTES),
        name="attn",
    )(qt, qb, qb, ka, vt, vt)


def _mixout_ffn_kernel(x_ref, ya_ref, ybt_ref, modm_ref, gnb_ref, gpostm_ref, wo_ref,
                       mod_ref, gpre_ref, gpost_ref, wg_ref, wu_ref, wd_ref, o_ref,
                       acc_ref, x2_ref, *, res_scale):
    d_a = ya_ref.shape[1]
    half = x_ref.shape[0] // 2

    def mix_out_half(h):
        rows = slice(h * half, (h + 1) * half)
        ybt = ybt_ref[:, rows].astype(F32)
        r = lax.rsqrt(jnp.mean(ybt * ybt, axis=0, keepdims=True) + EPS)
        gnb = jnp.concatenate([gnb_ref[...]] * (half // LANES), axis=1)
        yb_n = (ybt * r * gnb).T.astype(BF16)
        y = (jnp.dot(ya_ref[rows, :], wo_ref[:d_a, :], preferred_element_type=F32)
             + jnp.dot(yb_n, wo_ref[d_a:, :], preferred_element_type=F32))
        yn = y * _rms_scale(y) * gpostm_ref[...]
        gate = modm_ref[2:3, :]
        x2_ref[rows, :] = x_ref[rows, :] + (1.0 + gate) * yn

    _ffn_halves(x2_ref, mod_ref, gpre_ref, gpost_ref, wg_ref, wu_ref, wd_ref, o_ref, acc_ref,
                res_scale, before_half=mix_out_half)


def _mix_out_ffn(x, ya, ybt, mod, gnb_cols, g_post_mix, wo, g_pre, g_post, wg, wu, wd, res_scale):
    bsz, seq, d = x.shape
    d_a = ya.shape[2]
    d_b = ybt.shape[1]
    d_ff = wg.shape[1]
    tm = TM_FFN
    tok = lambda width: pl.BlockSpec((None, tm, width), lambda b, s: (b, s, 0))
    mod_spec = lambda sub: pl.BlockSpec((None, None, 3, d), lambda b, s: (b, sub, 0, 0))
    return pl.pallas_call(
        functools.partial(_mixout_ffn_kernel, res_scale=res_scale),
        grid=(bsz, seq // tm),
        in_specs=[tok(d), tok(d_a),
                  pl.BlockSpec((None, d_b, tm), lambda b, s: (b, 0, s)),
                  mod_spec(1),
                  _const_spec(gnb_cols.shape), _const_spec((1, d)), _const_spec(wo.shape),
                  mod_spec(2), _const_spec((1, d)), _const_spec((1, d)),
                  _const_spec((d, d_ff)), _const_spec((d, d_ff)), _const_spec((d_ff, d))],
        out_specs=tok(d),
        out_shape=jax.ShapeDtypeStruct(x.shape, F32),
        scratch_shapes=[pltpu.VMEM((2, tm // 2, d), F32), pltpu.VMEM((tm, d), F32)],
        compiler_params=pltpu.CompilerParams(
            dimension_semantics=("arbitrary", "arbitrary"),
            vmem_limit_bytes=VMEM_LIMIT_BYTES),
        name="mix_out_ffn",
    )(x, ya, ybt, mod, gnb_cols, g_post_mix.reshape(1, d), wo,
      mod, g_pre.reshape(1, d), g_post.reshape(1, d), wg, wu, wd)


def _key_bias_constants(n_heads):
    place = np.zeros((LANES, (n_heads // 2) * LANES), np.float32)
    ones = np.zeros((1, (n_heads // 2) * LANES), np.float32)
    for h in range(n_heads):
        base = (h // 2) * LANES + BIAS_SLOTS * (h % 2)
        for t in range(N_BIAS):
            place[t * SUBLANES + h, base + N_BIAS + t] = -1.0
            ones[0, base + t] = 1.0
    return jnp.asarray(place, BF16), jnp.asarray(ones, F32)


def kernel(x, c, w_ada, b_ada, norm_pre_g, norm_post_g, ffn_w_gate, ffn_w_up, ffn_w_down, w_in,
           sgu_ln_g, sgu_ln_b, sgu_w, sgu_b, fox_b_f, gnorm_a_g, gnorm_b_g, w_out):
    bsz, seq, d = x.shape
    depth = w_ada.shape[0]
    d_a = sgu_ln_g.shape[1]
    d_b = gnorm_b_g.shape[1]
    dh_a = d_a // H_A
    dh = d_b // H_B
    assert dh + BF16_SUBLANES == V_ROWS and TM_MIX == T_ATT and H_B == SUBLANES
    place, kones = _key_bias_constants(H_B)
    for l in range(depth):
        mod = _adaln(c, w_ada[l], b_ada[l]).reshape(bsz, N_SUB, 3, d)

        x = _ffn(x, mod, 0, norm_pre_g[l, 0], norm_post_g[l, 0],
                 ffn_w_gate[l, 0].astype(BF16), ffn_w_up[l, 0].astype(BF16),
                 ffn_w_down[l, 0].astype(BF16), 0.5)

        w_l = w_in[l]
        off_q = 2 * d_a
        wz = w_l[:, :off_q].astype(BF16)
        wq_t = w_l[:, off_q:off_q + d_b].T
        wv_t = w_l[:, off_q + 2 * d_b:off_q + 3 * d_b].T
        wf_t = jnp.pad(w_l[:, off_q + 3 * d_b:].T, ((0, BF16_SUBLANES - H_B), (0, 0)))
        wqvf = jnp.concatenate([wq_t, wv_t, wf_t], axis=0).astype(BF16)
        wk = w_l[:, off_q + d_b:off_q + 2 * d_b].astype(BF16)
        bf_rows = jnp.broadcast_to(fox_b_f[l][:, None], (H_B, TM_MIX))
        sgu_bias = jnp.repeat(sgu_b[l].T, dh_a, axis=1)
        ya, qt, qb, ka, vt = _mix_in(x, mod, norm_pre_g[l, 1], wz, wqvf, wk, bf_rows, place, kones,
                                     sgu_ln_g[l], sgu_ln_b[l], sgu_w[l], sgu_bias, gnorm_a_g[l])
        ybt = _attention(qt, qb, ka, vt)
        gnb_cols = jnp.broadcast_to(gnorm_b_g[l][:, None], (d_b, LANES))
        x = _mix_out_ffn(x, ya, ybt, mod, gnb_cols, norm_post_g[l, 1], w_out[l].astype(BF16),
                         norm_pre_g[l, 2], norm_post_g[l, 2],
                         ffn_w_gate[l, 1].astype(BF16), ffn_w_up[l, 1].astype(BF16),
                         ffn_w_down[l, 1].astype(BF16), 0.5)
    return x
```
